```python
import math
import jax
import jax.numpy as jnp
from jax import lax
import numpy as np

D_MODEL = 4096
BATCH = 2
SEQ = 8192
DEPTH = 4

GRID_W = 64
CTX_LEN = 256
HEAD_DIM = 128
EPS = 1e-6
NA_HEADS = (3 * D_MODEL) // (8 * HEAD_DIM)
NA_WIDTH = NA_HEADS * HEAD_DIM
NA_WIN_H = 8
NA_WIN_W = 16
GQA_Q_HEADS = (3 * D_MODEL) // (8 * HEAD_DIM)
GQA_KV_HEADS = GQA_Q_HEADS // 3
GQA_WIDTH = GQA_Q_HEADS * HEAD_DIM
GQA_KV_WIDTH = GQA_KV_HEADS * HEAD_DIM
Q_BLOCK = 128
ROPE_THETA = 10000.0
SSM_WIDTH = D_MODEL // 4
SSM_GROUP = 16
SSM_GROUPS = SSM_WIDTH // SSM_GROUP
SSM_STATE = 64
MIX_WIDTH = NA_WIDTH + SSM_WIDTH + GQA_WIDTH
IN_SPLITS = (NA_WIDTH, NA_WIDTH, NA_WIDTH, GQA_WIDTH, GQA_KV_WIDTH, GQA_KV_WIDTH, SSM_WIDTH, NA_WIDTH, SSM_WIDTH, GQA_WIDTH, D_MODEL, D_MODEL, D_MODEL)
IN_WIDTH = sum(IN_SPLITS)

kernel_name = 'hybrid_natten_s5_gqa_prefix_dit'


def rmsnorm(x, g):
    xf = x.astype(jnp.float32)
    y = xf * lax.rsqrt(jnp.mean(xf * xf, axis=-1, keepdims=True) + EPS)
    return (y * g.astype(jnp.float32)).astype(x.dtype)


def split_heads(t, n_heads):
    return t.reshape(t.shape[0], t.shape[1], n_heads, HEAD_DIM)


def split_projection(p):
    offsets = np.cumsum(np.array(IN_SPLITS))[:-1].tolist()
    return jnp.split(p, offsets, axis=-1)


def axial_rope_tables(length):
    t = jnp.arange(length)
    row = (t // GRID_W).astype(jnp.float32)
    col = (t % GRID_W).astype(jnp.float32)
    n_freq = HEAD_DIM // 4
    inv = jnp.power(ROPE_THETA, -jnp.arange(n_freq, dtype=jnp.float32) / n_freq)
    ang_r = row[:, None] * inv
    ang_c = col[:, None] * inv
    return jnp.cos(ang_r), jnp.sin(ang_r), jnp.cos(ang_c), jnp.sin(ang_c)


def _rotate(x, cos, sin):
    x1, x2 = jnp.split(x.astype(jnp.float32), 2, axis=-1)
    cos = cos[None, :, None, :]
    sin = sin[None, :, None, :]
    return jnp.concatenate([x1 * cos - x2 * sin, x2 * cos + x1 * sin], axis=-1)


def rope_2d(x, tables):
    cos_r, sin_r, cos_c, sin_c = tables
    half = HEAD_DIM // 2
    out = jnp.concatenate([_rotate(x[..., :half], cos_r, sin_r), _rotate(x[..., half:], cos_c, sin_c)], axis=-1)
    return out.astype(x.dtype)


def attend(q, k, v):
    s = jnp.einsum('bqhgd,bshd->bhgqs', q, k).astype(jnp.float32) * (HEAD_DIM ** -0.5)
    p = jax.nn.softmax(s, axis=-1).astype(v.dtype)
    return jnp.einsum('bhgqs,bshd->bqhgd', p, v)


def gqa_block_attention(q, k_all, v_all):
    b, l, hq, dh = q.shape
    hkv = k_all.shape[2]
    nb = l // Q_BLOCK
    qb = jnp.moveaxis(q.reshape(b, nb, Q_BLOCK, hkv, hq // hkv, dh), 1, 0)
    out = lax.map(lambda qq: attend(qq, k_all, v_all), qb)
    return jnp.moveaxis(out, 0, 1).reshape(b, l, hq * dh)


def neighbourhood_attention(q, k, v, k_ctx, v_ctx, rpb):
    b, l, h, dh = q.shape
    rows = l // GRID_W
    kh = min(NA_WIN_H, rows)
    n_loc = kh * NA_WIN_W
    qg = q.reshape(b, rows, GRID_W, h, dh)
    kg = k.reshape(b, rows, GRID_W, h, dh)
    vg = v.reshape(b, rows, GRID_W, h, dh)
    col = np.arange(GRID_W)
    col_start = np.clip(col - NA_WIN_W // 2, 0, GRID_W - NA_WIN_W)
    col_idx = col_start[:, None] + np.arange(NA_WIN_W)[None, :]
    rpb_col = rpb[:, :, col_idx - col[:, None] + NA_WIN_W - 1]
    scale = HEAD_DIM ** -0.5

    def row_block(r):
        rs = jnp.clip(r - kh // 2, 0, rows - kh)
        q_r = lax.dynamic_index_in_dim(qg, r, axis=1, keepdims=False)
        k_r = lax.dynamic_slice_in_dim(kg, rs, kh, axis=1)[:, :, col_idx]
        v_r = lax.dynamic_slice_in_dim(vg, rs, kh, axis=1)[:, :, col_idx]
        bias = jnp.take(rpb_col, rs + jnp.arange(kh) - r + NA_WIN_H - 1, axis=1).transpose(0, 2, 1, 3)
        s_loc = jnp.einsum('bqhd,bkqwhd->bhqkw', q_r, k_r).astype(jnp.float32) * scale + bias.astype(jnp.float32)
        s_ctx = jnp.einsum('bqhd,bchd->bhqc', q_r, k_ctx).astype(jnp.float32) * scale
        s = jnp.concatenate([s_loc.reshape(b, h, GRID_W, n_loc), s_ctx], axis=-1)
        p = jax.nn.softmax(s, axis=-1).astype(v.dtype)
        p_loc = p[..., :n_loc].reshape(b, h, GRID_W, kh, NA_WIN_W)
        return (jnp.einsum('bhqkw,bkqwhd->bqhd', p_loc, v_r)
                + jnp.einsum('bhqc,bchd->bqhd', p[..., n_loc:], v_ctx))

    out = lax.map(row_block, jnp.arange(rows))
    return jnp.moveaxis(out, 0, 1).reshape(b, l, h * dh)


def _linear_recurrence_combine(e1, e2):
    a1, b1 = e1
    a2, b2 = e2
    return a1 * a2, a2 * b1 + b2


def s5_discretise(a_re, a_im, log_dt, b_re, b_im):
    lam = lax.complex(a_re.astype(jnp.float32), a_im.astype(jnp.float32))
    dt = jnp.exp(log_dt.astype(jnp.float32))[:, None]
    lam_bar = jnp.exp(lam * dt)
    b_mat = lax.complex(b_re.astype(jnp.float32), b_im.astype(jnp.float32))
    b_bar = ((lam_bar - 1.0) / lam)[..., None] * b_mat
    return lam_bar, b_bar


def s5_scan(u, lam_bar, b_bar, h0, reverse):
    bu = jnp.einsum('blgi,gpi->blgp', u.astype(jnp.complex64), b_bar)
    if h0 is not None:
        edge = u.shape[1] - 1 if reverse else 0
        bu = bu.at[:, edge].add(lam_bar * h0)
    a = jnp.broadcast_to(lam_bar, bu.shape)
    _, s = lax.associative_scan(_linear_recurrence_combine, (a, bu), reverse=reverse, axis=1)
    return s


def s5_readout(s, c_mat):
    b, l = s.shape[:2]
    return jnp.real(jnp.einsum('blgp,gop->blgo', s, c_mat)).reshape(b, l, SSM_WIDTH)


def s5_glu(y, w_glu):
    y = jax.nn.gelu(y)
    a, g = jnp.split(y @ w_glu.astype(jnp.float32), 2, axis=-1)
    return a * jax.nn.sigmoid(g)


def s5_branch(u, u_ctx, a_re, a_im, log_dt, b_re, b_im, c_re, c_im, d, w_glu, need_ctx_out):
    b, l, _ = u.shape
    bc, lc, _ = u_ctx.shape
    uf = u.astype(jnp.float32)
    ucf = u_ctx.astype(jnp.float32)
    ug = uf.reshape(b, l, SSM_GROUPS, SSM_GROUP)
    ucg = ucf.reshape(bc, lc, SSM_GROUPS, SSM_GROUP)
    df = d.astype(jnp.float32)
    y = df * uf
    y_ctx = df * ucf if need_ctx_out else None
    for direction in range(2):
        reverse = direction == 1
        lam_bar, b_bar = s5_discretise(a_re[direction], a_im[direction], log_dt[direction], b_re[direction], b_im[direction])
        c_mat = lax.complex(c_re[direction].astype(jnp.float32), c_im[direction].astype(jnp.float32))
        s_ctx = s5_scan(ucg, lam_bar, b_bar, None, reverse)
        h0 = s_ctx[:, 0] if reverse else s_ctx[:, -1]
        s_lat = s5_scan(ug, lam_bar, b_bar, h0, reverse)
        y = y + s5_readout(s_lat, c_mat)
        if need_ctx_out:
            y_ctx = y_ctx + s5_readout(s_ctx, c_mat)
    out = s5_glu(y, w_glu).astype(u.dtype)
    out_ctx = s5_glu(y_ctx, w_glu).astype(u.dtype) if need_ctx_out else None
    return out, out_ctx


def merge_branches(oa, ob, oc, za, zb, zc, ga, gb, gc, w_branch, w_o):
    wa = w_branch[:NA_WIDTH]
    wb = w_branch[NA_WIDTH:NA_WIDTH + SSM_WIDTH]
    wc = w_branch[NA_WIDTH + SSM_WIDTH:]
    m = (jax.nn.sigmoid(ga) * ((oa * jax.nn.silu(za)) @ wa)
         + jax.nn.sigmoid(gb) * ((ob * jax.nn.silu(zb)) @ wb)
         + jax.nn.sigmoid(gc) * ((oc * jax.nn.silu(zc)) @ wc))
    return m @ w_o


def setup_inputs(seed: int = 0) -> dict:
    key = jax.random.key(seed)
    ks = jax.random.split(key, 24)
    f32 = jnp.float32

    def nrm(k, shape, s):
        return jax.random.normal(k, shape, f32) * s

    x = nrm(ks[0], (BATCH, SEQ, D_MODEL), 1.0)
    c = nrm(ks[1], (BATCH, D_MODEL), 1.0)
    ctx = nrm(ks[2], (BATCH, CTX_LEN, D_MODEL), 1.0)
    c_ctx = nrm(ks[3], (D_MODEL,), 1.0)
    norm_w = 1.0 + nrm(ks[4], (DEPTH, D_MODEL), 0.01)
    w_ada = nrm(ks[5], (DEPTH, D_MODEL, 3 * D_MODEL), 0.5 * D_MODEL ** -0.5)
    b_ada = nrm(ks[6], (DEPTH, 3 * D_MODEL), 0.01)
    w_in = nrm(ks[7], (DEPTH, D_MODEL, IN_WIDTH), D_MODEL ** -0.5)
    rpb = nrm(ks[8], (DEPTH, NA_HEADS, 2 * NA_WIN_H - 1, 2 * NA_WIN_W - 1), 0.02)
    q_gain = 1.0 + nrm(ks[9], (DEPTH, HEAD_DIM), 0.01)
    k_gain = 1.0 + nrm(ks[10], (DEPTH, HEAD_DIM), 0.01)
    ssm_a_re = -0.5 + nrm(ks[11], (DEPTH, 2, SSM_GROUPS, SSM_STATE), 0.01)
    ssm_a_im = jnp.pi * jnp.arange(SSM_STATE, dtype=f32) + nrm(ks[12], (DEPTH, 2, SSM_GROUPS, SSM_STATE), 0.01)
    ssm_log_dt = jax.random.uniform(ks[13], (DEPTH, 2, SSM_GROUPS), f32, math.log(1e-3), math.log(1e-1))
    ssm_b_re = nrm(ks[14], (DEPTH, 2, SSM_GROUPS, SSM_STATE, SSM_GROUP), (2 * SSM_GROUP) ** -0.5)
    ssm_b_im = nrm(ks[15], (DEPTH, 2, SSM_GROUPS, SSM_STATE, SSM_GROUP), (2 * SSM_GROUP) ** -0.5)
    ssm_c_re = nrm(ks[16], (DEPTH, 2, SSM_GROUPS, SSM_GROUP, SSM_STATE), 0.5)
    ssm_c_im = nrm(ks[17], (DEPTH, 2, SSM_GROUPS, SSM_GROUP, SSM_STATE), 0.5)
    ssm_d = nrm(ks[18], (DEPTH, SSM_WIDTH), 0.5)
    w_glu = nrm(ks[19], (DEPTH, SSM_WIDTH, 2 * SSM_WIDTH), SSM_WIDTH ** -0.5)
    w_branch = nrm(ks[20], (DEPTH, MIX_WIDTH, D_MODEL), MIX_WIDTH ** -0.5)
    w_o = nrm(ks[21], (DEPTH, D_MODEL, D_MODEL), D_MODEL ** -0.5)
    final_norm_w = 1.0 + nrm(ks[22], (D_MODEL,), 0.01)
    return {'x': x, 'c': c, 'ctx': ctx, 'c_ctx': c_ctx, 'norm_w': norm_w, 'w_ada': w_ada, 'b_ada': b_ada,
            'w_in': w_in, 'rpb': rpb, 'q_gain': q_gain, 'k_gain': k_gain, 'ssm_a_re': ssm_a_re,
            'ssm_a_im': ssm_a_im, 'ssm_log_dt': ssm_log_dt, 'ssm_b_re': ssm_b_re, 'ssm_b_im': ssm_b_im,
            'ssm_c_re': ssm_c_re, 'ssm_c_im': ssm_c_im, 'ssm_d': ssm_d, 'w_glu': w_glu,
            'w_branch': w_branch, 'w_o': w_o, 'final_norm_w': final_norm_w}


def reference(x, c, ctx, c_ctx, norm_w, w_ada, b_ada, w_in, rpb, q_gain, k_gain, ssm_a_re, ssm_a_im,
              ssm_log_dt, ssm_b_re, ssm_b_im, ssm_c_re, ssm_c_im, ssm_d, w_glu, w_branch, w_o, final_norm_w):
    b, l, _ = x.shape
    lc = ctx.shape[1]
    g_per_kv = GQA_Q_HEADS // GQA_KV_HEADS
    rope_tables = axial_rope_tables(l)
    sc = jax.nn.silu(c)
    scc = jax.nn.silu(c_ctx)
    h = x
    hc = ctx
    for layer in range(DEPTH):
        need_ctx_out = layer < DEPTH - 1
        shift, scale, gate = jnp.split((sc @ w_ada[layer] + b_ada[layer])[:, None, :], 3, axis=-1)
        shift_c, scale_c, gate_c = jnp.split(scc @ w_ada[layer] + b_ada[layer], 3, axis=-1)
        hn = rmsnorm(h, norm_w[layer]) * (1.0 + scale) + shift
        hcn = rmsnorm(hc, norm_w[layer]) * (1.0 + scale_c) + shift_c
        (qa, ka, va, qc, kc, vc, ub, za, zb, zc, ga, gb, gc) = split_projection(hn @ w_in[layer])
        (qa_x, ka_x, va_x, qc_x, kc_x, vc_x, ub_x, za_x, zb_x, zc_x, ga_x, gb_x, gc_x) = split_projection(hcn @ w_in[layer])

        ka_ctx = split_heads(ka_x, NA_HEADS)
        va_ctx = split_heads(va_x, NA_HEADS)
        oa = neighbourhood_attention(split_heads(qa, NA_HEADS), split_heads(ka, NA_HEADS), split_heads(va, NA_HEADS),
                                     ka_ctx, va_ctx, rpb[layer])

        q_lat = rope_2d(rmsnorm(split_heads(qc, GQA_Q_HEADS), q_gain[layer]), rope_tables)
        k_lat = rope_2d(rmsnorm(split_heads(kc, GQA_KV_HEADS), k_gain[layer]), rope_tables)
        k_ctx = rmsnorm(split_heads(kc_x, GQA_KV_HEADS), k_gain[layer])
        v_ctx = split_heads(vc_x, GQA_KV_HEADS)
        k_all = jnp.concatenate([k_ctx, k_lat], axis=1)
        v_all = jnp.concatenate([v_ctx, split_heads(vc, GQA_KV_HEADS)], axis=1)
        oc = gqa_block_attention(q_lat, k_all, v_all)

        ob, ob_ctx = s5_branch(ub, ub_x, ssm_a_re[layer], ssm_a_im[layer], ssm_log_dt[layer], ssm_b_re[layer],
                               ssm_b_im[layer], ssm_c_re[layer], ssm_c_im[layer], ssm_d[layer], w_glu[layer],
                               need_ctx_out)

        out = merge_branches(oa, ob, oc, za, zb, zc, ga, gb, gc, w_branch[layer], w_o[layer])
        if need_ctx_out:
            oa_c = attend(split_heads(qa_x, NA_HEADS)[:, :, :, None, :], ka_ctx, va_ctx).reshape(b, lc, NA_WIDTH)
            q_ctx = rmsnorm(split_heads(qc_x, GQA_Q_HEADS), q_gain[layer]).reshape(b, lc, GQA_KV_HEADS, g_per_kv, HEAD_DIM)
            oc_c = attend(q_ctx, k_ctx, v_ctx).reshape(b, lc, GQA_WIDTH)
            out_c = merge_branches(oa_c, ob_ctx, oc_c, za_x, zb_x, zc_x, ga_x, gb_x, gc_x, w_branch[layer], w_o[layer])
            hc = hc + gate_c * out_c
        h = h + gate * out
    return rmsnorm(h, final_norm_w)
```

```python
import functools
import math

import jax
import jax.numpy as jnp
import numpy as np
from jax import lax
from jax.experimental import pallas as pl
from jax.experimental.pallas import tpu as pltpu

HEAD_DIM = 128
GRID_W = 64
NA_WIN_H = 8
NA_WIN_W = 16
EPS = 1e-6
ROPE_THETA = 10000.0
SSM_CHUNK = 16
MASK_VALUE = -1e30

V7X_VMEM_LIMIT_BYTES = 56 * 1024 * 1024
V7X_MXU_WIDTH = 256

BF16 = jnp.bfloat16
F32 = jnp.float32


def _params(*sem):
    return pltpu.CompilerParams(dimension_semantics=sem, vmem_limit_bytes=V7X_VMEM_LIMIT_BYTES)


def _pick_tile(n, cap, mult):
    best = None
    for t in range(mult, min(n, cap) + 1, mult):
        if n % t == 0:
            best = t
    assert best is not None, (n, cap, mult)
    return best


def _col_tile(n, cap):
    for mult in (V7X_MXU_WIDTH, 128):
        if n % mult == 0:
            return _pick_tile(n, cap, mult)
    raise ValueError(n)


def _sigmoid(x):
    return 1.0 / (1.0 + jnp.exp(-x))


def _dot(a, b):
    return jnp.dot(a, b, preferred_element_type=F32)


def _dot_t(a, b):
    return lax.dot_general(a, b, (((1,), (1,)), ((), ())), preferred_element_type=F32)


def _ada_kernel(c_ref, w_ref, b_ref, o_ref):
    c = c_ref[...]
    a = (c * _sigmoid(c)).astype(BF16)
    o_ref[...] = _dot(a, w_ref[...].astype(BF16)) + b_ref[...]


def _ada_all_layers(cvec, w_ada, b_ada):
    depth, d, d3 = w_ada.shape
    tn = _pick_tile(d3, 512, 128)
    return pl.pallas_call(
        _ada_kernel,
        grid=(depth, d3 // tn),
        in_specs=[pl.BlockSpec((8, d), lambda l, j: (0, 0)),
                  pl.BlockSpec((None, d, tn), lambda l, j: (l, 0, j)),
                  pl.BlockSpec((None, 1, tn), lambda l, j: (l, 0, j))],
        out_specs=pl.BlockSpec((None, 8, tn), lambda l, j: (l, 0, j)),
        out_shape=jax.ShapeDtypeStruct((depth, 8, d3), F32),
        compiler_params=_params("parallel", "parallel"),
        name="ada_mod",
    )(cvec, w_ada, b_ada.reshape(depth, 1, d3))


def _normmod_kernel(h_ref, w_ref, mod_ref, o_ref, *, d):
    x = h_ref[...]
    ms = jnp.mean(x * x, axis=-1, keepdims=True)
    y = x * lax.rsqrt(ms + EPS) * w_ref[...]
    shift = mod_ref[:, 0:d]
    scale = mod_ref[:, d:2 * d]
    o_ref[...] = (y * (1.0 + scale) + shift).astype(o_ref.dtype)


def _norm_modulate(h, norm_w, mod4, layer, *, batch, ctx, s):
    n, d = h.shape
    rb = ctx
    nb = s // rb

    def mod_idx(i):
        return (layer, jnp.where(i % nb == 0, batch, i // nb), 0, 0)

    return pl.pallas_call(
        functools.partial(_normmod_kernel, d=d),
        grid=(n // rb,),
        in_specs=[pl.BlockSpec((rb, d), lambda i: (i, 0)),
                  pl.BlockSpec((None, 1, d), lambda i: (layer, 0, 0)),
                  pl.BlockSpec((None, None, 1, 3 * d), mod_idx)],
        out_specs=pl.BlockSpec((rb, d), lambda i: (i, 0)),
        out_shape=jax.ShapeDtypeStruct((n, d), BF16),
        compiler_params=_params("parallel"),
        name="norm_modulate",
    )(h, norm_w.reshape(norm_w.shape[0], 1, d), mod4)


def _mm_kernel(x_ref, w_ref, o_ref):
    o_ref[...] = _dot(x_ref[...], w_ref[...]).astype(o_ref.dtype)


def _matmul(x, w_all, layer, out_dtype, tm):
    m, k = x.shape
    nn = w_all.shape[2]
    tn = _col_tile(nn, 1024)
    return pl.pallas_call(
        _mm_kernel,
        grid=(m // tm, nn // tn),
        in_specs=[pl.BlockSpec((tm, k), lambda i, j: (i, 0)),
                  pl.BlockSpec((None, k, tn), lambda i, j: (layer, 0, j))],
        out_specs=pl.BlockSpec((tm, tn), lambda i, j: (i, j)),
        out_shape=jax.ShapeDtypeStruct((m, nn), out_dtype),
        compiler_params=_params("parallel", "arbitrary"),
        name="in_proj",
    )(x, w_all)


def _qkprep_kernel(x_ref, g_ref, cos_ref, sa_ref, sb_ref, o_ref, *, n_rot):
    cos = cos_ref[...]
    sa = sa_ref[...]
    sb = sb_ref[...]
    for hh in range(n_rot):
        lo = hh * HEAD_DIM
        x = x_ref[:, lo:lo + HEAD_DIM]
        ms = jnp.mean(x * x, axis=-1, keepdims=True)
        y = x * lax.rsqrt(ms + EPS) * g_ref[hh]
        r = y * cos + pltpu.roll(y, HEAD_DIM - HEAD_DIM // 4, 1) * sa + pltpu.roll(y, HEAD_DIM // 4, 1) * sb
        o_ref[:, lo:lo + HEAD_DIM] = r.astype(o_ref.dtype)
    lo = n_rot * HEAD_DIM
    o_ref[:, lo:] = x_ref[:, lo:].astype(o_ref.dtype)


def _qk_prep(pqkv, gains, cos_t, sa_t, sb_t, *, s):
    n, wtot = pqkv.shape
    n_rot = gains.shape[0]
    tr = _pick_tile(s, 528, 16)
    nsb = s // tr
    return pl.pallas_call(
        functools.partial(_qkprep_kernel, n_rot=n_rot),
        grid=(n // tr,),
        in_specs=[pl.BlockSpec((tr, wtot), lambda i: (i, 0)),
                  pl.BlockSpec((n_rot, 1, HEAD_DIM), lambda i: (0, 0, 0)),
                  pl.BlockSpec((tr, HEAD_DIM), lambda i: (i % nsb, 0)),
                  pl.BlockSpec((tr, HEAD_DIM), lambda i: (i % nsb, 0)),
                  pl.BlockSpec((tr, HEAD_DIM), lambda i: (i % nsb, 0))],
        out_specs=pl.BlockSpec((tr, wtot), lambda i: (i, 0)),
        out_shape=jax.ShapeDtypeStruct((n, wtot), BF16),
        compiler_params=_params("parallel"),
        name="gqa_qk_prep",
    )(pqkv, gains, cos_t, sa_t, sb_t)


def _gqa_kernel(q_ref, k_ref, v_ref, o_ref, m_sc, l_sc, acc_sc, *, tq, tk, n_chunks, ctx, gsz):
    qi = pl.program_id(2)
    q3 = jnp.concatenate([q_ref[:, j * HEAD_DIM:(j + 1) * HEAD_DIM] for j in range(gsz)], axis=0)

    def finish(acc, l):
        o = acc / l
        for j in range(gsz):
            o_ref[:, j * HEAD_DIM:(j + 1) * HEAD_DIM] = o[j * tq:(j + 1) * tq].astype(o_ref.dtype)

    @pl.when(qi == 0)
    def _ctx_queries():
        kc = k_ref[0:ctx, :]
        vc = v_ref[0:ctx, :]
        sc = _dot_t(q3, kc)
        m = jnp.max(sc, axis=-1, keepdims=True)
        p = jnp.exp(sc - m)
        l = jnp.sum(p, axis=-1, keepdims=True)
        finish(_dot(p.astype(BF16), vc), l)

    @pl.when(qi > 0)
    def _latent_queries():
        m_sc[...] = jnp.full(m_sc.shape, -jnp.inf, F32)
        l_sc[...] = jnp.zeros(l_sc.shape, F32)
        acc_sc[...] = jnp.zeros(acc_sc.shape, F32)

        def body(c, carry):
            off = pl.multiple_of(c * tk, tk)
            kc = k_ref[pl.ds(off, tk), :]
            vc = v_ref[pl.ds(off, tk), :]
            sc = _dot_t(q3, kc)
            m_prev = m_sc[...]
            m_new = jnp.maximum(m_prev, jnp.max(sc, axis=-1, keepdims=True))
            alpha = jnp.exp(m_prev - m_new)
            p = jnp.exp(sc - m_new[:, 0:1])
            l_sc[...] = alpha * l_sc[...] + jnp.sum(p, axis=-1, keepdims=True)
            acc_sc[...] = alpha * acc_sc[...] + _dot(p.astype(BF16), vc)
            m_sc[...] = m_new
            return carry

        lax.fori_loop(0, n_chunks, body, 0)
        finish(acc_sc[...], l_sc[...])


def _gqa_attention(qkv, *, batch, s, ctx, nq, nk):
    gsz = nq // nk
    tq = ctx
    tk = _pick_tile(s, 1024, ctx)
    gw = gsz * HEAD_DIM
    kern = functools.partial(_gqa_kernel, tq=tq, tk=tk, n_chunks=s // tk, ctx=ctx, gsz=gsz)
    return pl.pallas_call(
        kern,
        grid=(batch, nk, s // tq),
        in_specs=[pl.BlockSpec((None, tq, gw), lambda b, g, i: (b, i, g)),
                  pl.BlockSpec((None, s, HEAD_DIM), lambda b, g, i: (b, 0, nq + g)),
                  pl.BlockSpec((None, s, HEAD_DIM), lambda b, g, i: (b, 0, nq + nk + g))],
        out_specs=pl.BlockSpec((None, tq, gw), lambda b, g, i: (b, i, g)),
        out_shape=jax.ShapeDtypeStruct((batch, s, nq * HEAD_DIM), F32),
        scratch_shapes=[pltpu.VMEM((gsz * tq, HEAD_DIM), F32),
                        pltpu.VMEM((gsz * tq, HEAD_DIM), F32),
                        pltpu.VMEM((gsz * tq, HEAD_DIM), F32)],
        compiler_params=_params("parallel", "parallel", "arbitrary"),
        name="gqa_attention",
    )(qkv, qkv, qkv)


def _na_kernel(q_ref, k_ref, v_ref, tab_ref, o_ref, *, ctx, rows, scale):
    kc = k_ref[0:ctx, :]
    vc = v_ref[0:ctx, :]

    sc = _dot_t(q_ref[0:ctx, :], kc) * scale
    m = jnp.max(sc, axis=-1, keepdims=True)
    p = jnp.exp(sc - m)
    l = jnp.sum(p, axis=-1, keepdims=True)
    o_ref[0:ctx, :] = (_dot(p.astype(BF16), vc) / l).astype(o_ref.dtype)

    n_loc = NA_WIN_H * GRID_W

    def body(r, carry):
        rs = jnp.clip(r - NA_WIN_H // 2, 0, rows - NA_WIN_H)
        d0 = rs - r + (NA_WIN_H - 1)
        qs = pl.multiple_of(ctx + r * GRID_W, GRID_W)
        ks = pl.multiple_of(ctx + rs * GRID_W, GRID_W)
        q = q_ref[pl.ds(qs, GRID_W), :]
        kl = k_ref[pl.ds(ks, n_loc), :]
        vl = v_ref[pl.ds(ks, n_loc), :]
        s1 = _dot_t(q, kl) * scale + tab_ref[d0]
        s2 = _dot_t(q, kc) * scale
        mm = jnp.maximum(jnp.max(s1, axis=-1, keepdims=True), jnp.max(s2, axis=-1, keepdims=True))
        p1 = jnp.exp(s1 - mm)
        p2 = jnp.exp(s2 - mm)
        ll = jnp.sum(p1, axis=-1, keepdims=True) + jnp.sum(p2, axis=-1, keepdims=True)
        o = (_dot(p1.astype(BF16), vl) + _dot(p2.astype(BF16), vc)) / ll
        o_ref[pl.ds(qs, GRID_W), :] = o.astype(o_ref.dtype)
        return carry

    lax.fori_loop(0, rows, body, 0)


def _na_bias_table(rpb_l):
    col = np.arange(GRID_W)
    col_start = np.clip(col - NA_WIN_W // 2, 0, GRID_W - NA_WIN_W)
    kc = np.arange(GRID_W)
    valid = (kc[None, :] >= col_start[:, None]) & (kc[None, :] < col_start[:, None] + NA_WIN_W)
    rel = np.clip(kc[None, :] - col[:, None] + NA_WIN_W - 1, 0, 2 * NA_WIN_W - 2)
    d0 = np.arange(NA_WIN_H)
    i = np.arange(NA_WIN_H)
    row_idx = d0[:, None] + i[None, :]
    t = rpb_l[:, row_idx][:, :, :, rel]
    t = jnp.where(valid[None, None, None], t, MASK_VALUE)
    t = jnp.transpose(t, (0, 1, 3, 2, 4))
    return t.reshape(t.shape[0], NA_WIN_H, GRID_W, NA_WIN_H * GRID_W).astype(F32)


def _na_attention(qkv, tab, *, batch, s, ctx, nh, rows):
    kern = functools.partial(_na_kernel, ctx=ctx, rows=rows, scale=HEAD_DIM ** -0.5)
    return pl.pallas_call(
        kern,
        grid=(batch, nh),
        in_specs=[pl.BlockSpec((None, s, HEAD_DIM), lambda b, h: (b, 0, h)),
                  pl.BlockSpec((None, s, HEAD_DIM), lambda b, h: (b, 0, nh + h)),
                  pl.BlockSpec((None, s, HEAD_DIM), lambda b, h: (b, 0, 2 * nh + h)),
                  pl.BlockSpec((None, NA_WIN_H, GRID_W, NA_WIN_H * GRID_W), lambda b, h: (h, 0, 0, 0))],
        out_specs=pl.BlockSpec((None, s, HEAD_DIM), lambda b, h: (b, 0, h)),
        out_shape=jax.ShapeDtypeStruct((batch, s, nh * HEAD_DIM), F32),
        compiler_params=_params("parallel", "parallel"),
        name="na_attention",
    )(qkv, qkv, qkv, tab)


def _s5_kernel(u_ref, m_ref, s_ref, r_ref, lam_ref, y_ref, xl_sc, xp_sc, *, pairs, batch, nck, ctxc):
    w = 2 * 64
    nckt, ctxt = nck // 8, ctxc // 8
    for pp in range(pairs):
        ub = u_ref[pp].astype(BF16)
        xl_sc[...] = _dot(ub, s_ref[pp])
        lam = lam_ref[pp]
        lrf, lif, lrr, lir = lam[0:1, :], lam[1:2, :], lam[2:3, :], lam[3:4, :]

        sub = lax.broadcasted_iota(jnp.int32, (8, w), 0)

        def body(it, carry):
            jt = jnp.where(it < ctxt, ctxt - 1 - it, nckt - 1 - (it - ctxt))
            new = []
            for b in range(batch):
                fr, fi, rr, ri = carry[4 * b:4 * b + 4]
                row_f = pl.multiple_of(b * nck + it * 8, 8)
                row_r = pl.multiple_of(b * nck + jt * 8, 8)
                xf = xl_sc[pl.ds(row_f, 8), 0:2 * w]
                xr = xl_sc[pl.ds(row_r, 8), 2 * w:4 * w]
                pfr = pfi = prr = pri = jnp.zeros((8, w), F32)
                for k in range(8):
                    pfr = jnp.where(sub == k, fr, pfr)
                    pfi = jnp.where(sub == k, fi, pfi)
                    fr, fi = (lrf * fr - lif * fi + xf[k:k + 1, 0:w],
                              lrf * fi + lif * fr + xf[k:k + 1, w:2 * w])
                    kr = 7 - k
                    prr = jnp.where(sub == kr, rr, prr)
                    pri = jnp.where(sub == kr, ri, pri)
                    rr, ri = (lrr * rr - lir * ri + xr[kr:kr + 1, 0:w],
                              lrr * ri + lir * rr + xr[kr:kr + 1, w:2 * w])
                xp_sc[pl.ds(row_f, 8), 0:w] = pfr
                xp_sc[pl.ds(row_f, 8), w:2 * w] = pfi
                xp_sc[pl.ds(row_r, 8), 2 * w:3 * w] = prr
                xp_sc[pl.ds(row_r, 8), 3 * w:4 * w] = pri
                new += [fr, fi, rr, ri]
            return tuple(new)

        zero = jnp.zeros((1, w), F32)
        lax.fori_loop(0, nckt, body, tuple(zero for _ in range(4 * batch)))
        y_ref[pp] = _dot(ub, m_ref[pp]) + _dot(xp_sc[...].astype(BF16), r_ref[pp])


def _s5_mix(u_pairs, m_p, s_p, r_p, lam_p, *, batch, nck, ctxc):
    npairs, nrow, wid = u_pairs.shape
    pairs = _pick_tile(npairs, 2, 1)
    kern = functools.partial(_s5_kernel, pairs=pairs, batch=batch, nck=nck, ctxc=ctxc)
    mat_spec = pl.BlockSpec((pairs, wid, wid), lambda g: (g, 0, 0))
    return pl.pallas_call(
        kern,
        grid=(npairs // pairs,),
        in_specs=[pl.BlockSpec((pairs, nrow, wid), lambda g: (g, 0, 0)),
                  mat_spec, mat_spec, mat_spec,
                  pl.BlockSpec((pairs, 8, 128), lambda g: (g, 0, 0))],
        out_specs=pl.BlockSpec((pairs, nrow, wid), lambda g: (g, 0, 0)),
        out_shape=jax.ShapeDtypeStruct((npairs, nrow, wid), F32),
        scratch_shapes=[pltpu.VMEM((nrow, wid), F32), pltpu.VMEM((nrow, wid), F32)],
        compiler_params=_params("parallel"),
        name="s5_mix",
    )(u_pairs, m_p, s_p, r_p, lam_p)


def _s5_matrices(a_re, a_im, log_dt, b_re, b_im, c_re, c_im, d_skip):
    q = SSM_CHUNK
    hi = lax.Precision.HIGHEST
    g, p = a_re.shape[1], a_re.shape[2]
    sg = b_re.shape[3]
    assert 2 * p == 128 and q * sg == 256
    pw = np.arange(q + 1, dtype=np.float32)
    ms, ss, rs, lams = [], [], [], []
    tt = np.arange(q)
    for direction in range(2):
        lam = lax.complex(a_re[direction].astype(F32), a_im[direction].astype(F32))
        dt = jnp.exp(log_dt[direction].astype(F32))[:, None]
        z = lam * dt
        lam_bar = jnp.exp(z)
        b_bar = ((lam_bar - 1.0) / lam)[..., None] * lax.complex(b_re[direction].astype(F32), b_im[direction].astype(F32))
        c_mat = lax.complex(c_re[direction].astype(F32), c_im[direction].astype(F32))
        lpow = jnp.exp(z[None] * pw[:, None, None])
        wmat = c_mat[:, None, :, :] * jnp.transpose(lpow[:q], (1, 0, 2))[:, :, None, :]
        kk = (jnp.einsum('gtop,gpi->gtoi', jnp.real(wmat), jnp.real(b_bar), precision=hi)
              - jnp.einsum('gtop,gpi->gtoi', jnp.imag(wmat), jnp.imag(b_bar), precision=hi))
        if direction == 0:
            lag = tt[None, :] - tt[:, None]
            s_pow = q - 1 - tt
            r_pow = tt + 1
        else:
            lag = tt[:, None] - tt[None, :]
            s_pow = tt
            r_pow = q - tt
        mk = kk[:, np.clip(lag, 0, q - 1)]
        mk = jnp.where((lag >= 0)[None, :, :, None, None], mk, 0.0)
        ms.append(jnp.transpose(mk, (0, 1, 4, 2, 3)))
        sm = jnp.transpose(lpow[s_pow], (1, 0, 2))[:, :, None, :] * jnp.transpose(b_bar, (0, 2, 1))[:, None, :, :]
        ss.append((jnp.real(sm), jnp.imag(sm)))
        rm = jnp.transpose(lpow[r_pow], (1, 2, 0))[:, :, :, None] * jnp.transpose(c_mat, (0, 2, 1))[:, :, None, :]
        rs.append((jnp.real(rm), -jnp.imag(rm)))
        lams.append((jnp.real(lpow[q]), jnp.imag(lpow[q])))
    eye_q = jnp.eye(q, dtype=F32)
    eye_i = jnp.eye(sg, dtype=F32)
    dm = d_skip.astype(F32).reshape(g, 1, sg, 1, 1) * eye_q[None, :, None, :, None] * eye_i[None, None, :, None, :]
    m_tot = (ms[0] + ms[1] + dm).reshape(g, q * sg, q * sg)
    eye2 = jnp.eye(2, dtype=F32)
    npair = g // 2

    def pair_diag(x):
        r, c = x.shape[1], x.shape[2]
        x = x.reshape(npair, 2, r, c)
        return (x[:, :, :, None, :] * eye2[None, :, None, :, None]).reshape(npair, 2 * r, 2 * c)

    m_p = pair_diag(m_tot)
    s_p = jnp.concatenate([pair_diag(part.reshape(g, q * sg, p)) for pr in ss for part in pr], axis=2)
    r_p = jnp.concatenate([pair_diag(part.reshape(g, p, q * sg)) for pr in rs for part in pr], axis=1)
    lam_rows = jnp.stack([part.reshape(npair, 2 * p) for pr in lams for part in pr], axis=1)
    lam_p = jnp.concatenate([lam_rows, jnp.zeros((npair, 4, 2 * p), F32)], axis=1)
    return m_p.astype(BF16), s_p.astype(BF16), r_p.astype(BF16), lam_p


def _glu_kernel(y_ref, w_ref, o_ref, *, n):
    y = y_ref[...]
    cdf = 0.5 * (1.0 + jnp.tanh(math.sqrt(2.0 / math.pi) * (y + 0.044715 * (y * y * y))))
    r = _dot((y * cdf).astype(BF16), w_ref[...])
    o_ref[...] = (r[:, :n] * _sigmoid(r[:, n:])).astype(o_ref.dtype)


def _s5_glu(y, w_glu_all, layer, tm):
    m, n = y.shape
    return pl.pallas_call(
        functools.partial(_glu_kernel, n=n),
        grid=(m // tm,),
        in_specs=[pl.BlockSpec((tm, n), lambda i: (i, 0)),
                  pl.BlockSpec((None, n, 2 * n), lambda i: (layer, 0, 0))],
        out_specs=pl.BlockSpec((tm, n), lambda i: (i, 0)),
        out_shape=jax.ShapeDtypeStruct((m, n), F32),
        compiler_params=_params("parallel"),
        name="s5_glu",
    )(y, w_glu_all)


def _gate_kernel(oa_ref, ob_ref, oc_ref, z_ref, o_ref, *, wa, wb):
    def gated(o, z):
        return (o * (z * _sigmoid(z))).astype(o_ref.dtype)

    o_ref[:, 0:wa] = gated(oa_ref[...], z_ref[:, 0:wa])
    o_ref[:, wa:wa + wb] = gated(ob_ref[...], z_ref[:, wa:wa + wb])
    o_ref[:, wa + wb:] = gated(oc_ref[...], z_ref[:, wa + wb:])


def _branch_gate(oa, ob, oc, zg, *, s):
    n, wa = oa.shape
    wb = ob.shape[1]
    mix = wa + wb + oc.shape[1]
    tr = _pick_tile(s, 528, 16)
    return pl.pallas_call(
        functools.partial(_gate_kernel, wa=wa, wb=wb),
        grid=(n // tr,),
        in_specs=[pl.BlockSpec((tr, wa), lambda i: (i, 0)),
                  pl.BlockSpec((tr, wb), lambda i: (i, 0)),
                  pl.BlockSpec((tr, oc.shape[1]), lambda i: (i, 0)),
                  pl.BlockSpec((tr, mix), lambda i: (i, 0))],
        out_specs=pl.BlockSpec((tr, mix), lambda i: (i, 0)),
        out_shape=jax.ShapeDtypeStruct((n, mix), BF16),
        compiler_params=_params("parallel"),
        name="branch_gate",
    )(oa, ob, oc, zg)


def _merge_kernel(a_ref, w_ref, ga_ref, gb_ref, gc_ref, o_ref, *, wa, wb):
    ma = _dot(a_ref[:, 0:wa], w_ref[0:wa, :])
    mb = _dot(a_ref[:, wa:wa + wb], w_ref[wa:wa + wb, :])
    mc = _dot(a_ref[:, wa + wb:], w_ref[wa + wb:, :])
    m = _sigmoid(ga_ref[...]) * ma + _sigmoid(gb_ref[...]) * mb + _sigmoid(gc_ref[...]) * mc
    o_ref[...] = m.astype(o_ref.dtype)


def _branch_merge(acat, w_branch_all, zg, layer, *, wa, wb, d, tm):
    n, mix = acat.shape
    tn = _col_tile(d, 512)
    g0 = mix // tn
    gd = d // tn
    return pl.pallas_call(
        functools.partial(_merge_kernel, wa=wa, wb=wb),
        grid=(n // tm, d // tn),
        in_specs=[pl.BlockSpec((tm, mix), lambda i, j: (i, 0)),
                  pl.BlockSpec((None, mix, tn), lambda i, j: (layer, 0, j)),
                  pl.BlockSpec((tm, tn), lambda i, j: (i, g0 + j)),
                  pl.BlockSpec((tm, tn), lambda i, j: (i, g0 + gd + j)),
                  pl.BlockSpec((tm, tn), lambda i, j: (i, g0 + 2 * gd + j))],
        out_specs=pl.BlockSpec((tm, tn), lambda i, j: (i, j)),
        out_shape=jax.ShapeDtypeStruct((n, d), BF16),
        compiler_params=_params("parallel", "arbitrary"),
        name="branch_merge",
    )(acat, w_branch_all, zg, zg, zg)


def _outproj_kernel(m_ref, w_ref, h_ref, gb_ref, gc_ref, o_ref, *, d, tm, tn, ctx, blocks_per_batch):
    i = pl.program_id(0)
    j = pl.program_id(1)
    out = _dot(m_ref[...], w_ref[...])
    lo = pl.multiple_of(2 * d + j * tn, tn)
    gate_b = gb_ref[:, pl.ds(lo, tn)]
    gate_c = gc_ref[:, pl.ds(lo, tn)]
    row = lax.broadcasted_iota(jnp.int32, (tm, 1), 0) + (i % blocks_per_batch) * tm
    gate = jnp.where(row < ctx, gate_c, gate_b)
    o_ref[...] = h_ref[...] + gate * out


def _out_project_residual(m, w_o_all, h, mod4, layer, *, batch, s, ctx, tm):
    n, d = h.shape
    tn = _col_tile(d, 512)
    bpb = s // tm
    kern = functools.partial(_outproj_kernel, d=d, tm=tm, tn=tn, ctx=ctx, blocks_per_batch=bpb)
    return pl.pallas_call(
        kern,
        grid=(n // tm, d // tn),
        in_specs=[pl.BlockSpec((tm, d), lambda i, j: (i, 0)),
                  pl.BlockSpec((None, d, tn), lambda i, j: (layer, 0, j)),
                  pl.BlockSpec((tm, tn), lambda i, j: (i, j)),
                  pl.BlockSpec((None, None, 1, 3 * d), lambda i, j: (layer, i // bpb, 0, 0)),
                  pl.BlockSpec((None, None, 1, 3 * d), lambda i, j: (layer, batch, 0, 0))],
        out_specs=pl.BlockSpec((tm, tn), lambda i, j: (i, j)),
        out_shape=jax.ShapeDtypeStruct((n, d), F32),
        input_output_aliases={2: 0},
        compiler_params=_params("parallel", "arbitrary"),
        name="out_proj_residual",
    )(m, w_o_all, h, mod4, mod4)


def _final_norm_kernel(h_ref, w_ref, o_ref):
    x = h_ref[...]
    ms = jnp.mean(x * x, axis=-1, keepdims=True)
    o_ref[...] = x * lax.rsqrt(ms + EPS) * w_ref[...]


def _final_norm(h3, w, *, ctx, l):
    batch, s, d = h3.shape
    rb = ctx
    off = ctx // rb
    return pl.pallas_call(
        _final_norm_kernel,
        grid=(batch, l // rb),
        in_specs=[pl.BlockSpec((None, rb, d), lambda b, i: (b, off + i, 0)),
                  pl.BlockSpec((1, d), lambda b, i: (0, 0))],
        out_specs=pl.BlockSpec((None, rb, d), lambda b, i: (b, i, 0)),
        out_shape=jax.ShapeDtypeStruct((batch, l, d), F32),
        compiler_params=_params("parallel", "parallel"),
        name="final_norm",
    )(h3, w.reshape(1, d))


def _rope_tables(l, ctx):
    t = np.arange(l)
    row = (t // GRID_W).astype(np.float32)
    col = (t % GRID_W).astype(np.float32)
    n_freq = HEAD_DIM // 4
    inv = jnp.power(ROPE_THETA, -jnp.arange(n_freq, dtype=F32) / n_freq)
    ang_r = jnp.asarray(row)[:, None] * inv
    ang_c = jnp.asarray(col)[:, None] * inv
    cr, sr, cc, sn = jnp.cos(ang_r), jnp.sin(ang_r), jnp.cos(ang_c), jnp.sin(ang_c)
    zero = jnp.zeros_like(sr)
    cos_t = jnp.concatenate([cr, cr, cc, cc], axis=1)
    sa_t = jnp.concatenate([-sr, zero, -sn, zero], axis=1)
    sb_t = jnp.concatenate([zero, sr, zero, sn], axis=1)
    ident = jnp.ones((ctx, HEAD_DIM), F32)
    nil = jnp.zeros((ctx, HEAD_DIM), F32)
    return (jnp.concatenate([ident, cos_t], axis=0), jnp.concatenate([nil, sa_t], axis=0),
            jnp.concatenate([nil, sb_t], axis=0))


def kernel(x, c, ctx, c_ctx, norm_w, w_ada, b_ada, w_in, rpb, q_gain, k_gain, ssm_a_re, ssm_a_im, ssm_log_dt,
           ssm_b_re, ssm_b_im, ssm_c_re, ssm_c_im, ssm_d, w_glu, w_branch, w_o, final_norm_w):
    batch, l, d = x.shape
    lc = ctx.shape[1]
    depth = w_in.shape[0]
    s = lc + l
    n = batch * s
    rows = l // GRID_W
    nh_a = rpb.shape[1]
    na_w = nh_a * HEAD_DIM
    nq = (3 * d) // (8 * HEAD_DIM)
    nk = nq // 3
    gq_w = nq * HEAD_DIM
    kv_w = nk * HEAD_DIM
    ssm_w = ssm_d.shape[1]
    n_groups = ssm_a_re.shape[2]
    mix = na_w + ssm_w + gq_w
    assert l % GRID_W == 0 and rows >= NA_WIN_H and l % lc == 0 and lc % (8 * SSM_CHUNK) == 0 and batch < 8
    assert w_in.shape[2] == 3 * na_w + gq_w + 2 * kv_w + ssm_w + mix + 3 * d
    tm = _pick_tile(s, 1056, 16)

    o_qkv_c = 3 * na_w
    o_ub = o_qkv_c + gq_w + 2 * kv_w
    o_z = o_ub + ssm_w
    w_in_a = w_in[:, :, 0:o_qkv_c].astype(BF16)
    w_in_c = w_in[:, :, o_qkv_c:o_ub].astype(BF16)
    w_in_u = w_in[:, :, o_ub:o_z].astype(BF16)
    w_in_zg = w_in[:, :, o_z:].astype(BF16)
    w_glu_b = w_glu.astype(BF16)
    w_branch_b = w_branch.astype(BF16)
    w_o_b = w_o.astype(BF16)

    cvec = jnp.concatenate([c, c_ctx[None, :], jnp.zeros((8 - batch - 1, d), F32)], axis=0)
    mod = _ada_all_layers(cvec, w_ada, b_ada)
    mod4 = mod.reshape(depth, 8, 1, 3 * d)

    cos_t, sa_t, sb_t = _rope_tables(l, lc)
    nck = s // SSM_CHUNK
    ctxc = lc // SSM_CHUNK
    npair = n_groups // 2
    sg = ssm_w // n_groups

    h = jnp.concatenate([ctx, x], axis=1).reshape(n, d)
    for layer in range(depth):
        hn = _norm_modulate(h, norm_w, mod4, layer, batch=batch, ctx=lc, s=s)
        p_a = _matmul(hn, w_in_a, layer, BF16, tm)
        p_c = _matmul(hn, w_in_c, layer, F32, tm)
        p_u = _matmul(hn, w_in_u, layer, F32, tm)
        p_zg = _matmul(hn, w_in_zg, layer, F32, tm)

        tab = _na_bias_table(rpb[layer])
        oa = _na_attention(p_a.reshape(batch, s, 3 * na_w), tab, batch=batch, s=s, ctx=lc, nh=nh_a, rows=rows)

        gains = jnp.concatenate([jnp.broadcast_to(q_gain[layer] * (HEAD_DIM ** -0.5), (nq, HEAD_DIM)),
                                 jnp.broadcast_to(k_gain[layer], (nk, HEAD_DIM))], axis=0).reshape(nq + nk, 1, HEAD_DIM)
        qkv = _qk_prep(p_c, gains, cos_t, sa_t, sb_t, s=s)
        oc = _gqa_attention(qkv.reshape(batch, s, gq_w + 2 * kv_w), batch=batch, s=s, ctx=lc, nq=nq, nk=nk)

        m_p, s_p, r_p, lam_p = _s5_matrices(ssm_a_re[layer], ssm_a_im[layer], ssm_log_dt[layer], ssm_b_re[layer],
                                            ssm_b_im[layer], ssm_c_re[layer], ssm_c_im[layer], ssm_d[layer])
        u_pairs = jnp.transpose(p_u.reshape(batch, nck, SSM_CHUNK, npair, 2, sg), (3, 0, 1, 4, 2, 5))
        u_pairs = u_pairs.reshape(npair, batch * nck, 2 * SSM_CHUNK * sg)
        y_pairs = _s5_mix(u_pairs, m_p, s_p, r_p, lam_p, batch=batch, nck=nck, ctxc=ctxc)
        y = jnp.transpose(y_pairs.reshape(npair, batch, nck, 2, SSM_CHUNK, sg), (1, 2, 4, 0, 3, 5)).reshape(n, ssm_w)
        ob = _s5_glu(y, w_glu_b, layer, tm)

        acat = _branch_gate(oa.reshape(n, na_w), ob, oc.reshape(n, gq_w), p_zg, s=s)
        m = _branch_merge(acat, w_branch_b, p_zg, layer, wa=na_w, wb=ssm_w, d=d, tm=tm)
        h = _out_project_residual(m, w_o_b, h, mod4, layer, batch=batch, s=s, ctx=lc, tm=tm)

    return _final_norm(h.reshape(batch, s, d), final_norm_w, ctx=lc, l=l)
```
